```python
import math
import jax, jax.numpy as jnp
from jax import lax
import numpy as np

D_MODEL = 1024
BATCH = 32
SEQ = 2048
DEPTH = 1

CHUNK = 64
D_MIX = 2 * D_MODEL
SSD_WIDTH = D_MIX // 2
SSD_HEAD_DIM = 64
SSD_HEADS = SSD_WIDTH // SSD_HEAD_DIM
SSD_GROUPS = 2
SSD_STATE = 128
SSD_HEADS_PER_GROUP = SSD_HEADS // SSD_GROUPS
CONV_WIDTH = 4
CONV_DIM = SSD_WIDTH + 2 * SSD_GROUPS * SSD_STATE
ATT_WIDTH = D_MIX - SSD_WIDTH
ATT_HEAD_DIM = 64
ATT_HEADS = ATT_WIDTH // (2 * ATT_HEAD_DIM)
ROT_DIM = ATT_HEAD_DIM // 4
ROPE_THETA = 500000.0
Q_BLOCK = 128
IN_DIM = SSD_WIDTH + CONV_DIM + SSD_HEADS + 3 * ATT_WIDTH
SPLITS = (SSD_WIDTH,
          SSD_WIDTH + CONV_DIM,
          SSD_WIDTH + CONV_DIM + SSD_HEADS,
          SSD_WIDTH + CONV_DIM + SSD_HEADS + ATT_WIDTH,
          SSD_WIDTH + CONV_DIM + SSD_HEADS + 2 * ATT_WIDTH)
D_FF = -(-8 * D_MODEL // (3 * 256)) * 256
EPS = 1e-6

kernel_name = "hybrid_ssd_diffattn_swiglu_layer"


def rms_norm(x, w):
    xf = x.astype(jnp.float32)
    y = xf * lax.rsqrt(jnp.mean(xf * xf, axis=-1, keepdims=True) + EPS)
    return (y * w.astype(jnp.float32)).astype(x.dtype)


def causal_depthwise_conv(x, w, b):
    y = lax.conv_general_dilated(x, w[:, None, :], window_strides=(1,),
                                 padding=[(CONV_WIDTH - 1, 0)],
                                 dimension_numbers=("NWC", "WIO", "NWC"),
                                 feature_group_count=x.shape[-1])
    return y + b


def ssd_mixer(z, xbc, dt_raw, conv_w, conv_b, dt_bias, a_log, d_skip, norm_w):
    b, s, _ = xbc.shape
    G, R, P, N = SSD_GROUPS, SSD_HEADS_PER_GROUP, SSD_HEAD_DIM, SSD_STATE
    xbc = jax.nn.silu(causal_depthwise_conv(xbc, conv_w, conv_b))
    xs = xbc[..., :SSD_WIDTH]
    bm = xbc[..., SSD_WIDTH:SSD_WIDTH + G * N]
    cm = xbc[..., SSD_WIDTH + G * N:]
    xh = xs.astype(jnp.float32).reshape(b, s, G, R, P)
    bm = bm.astype(jnp.float32).reshape(b, s, G, N)
    cm = cm.astype(jnp.float32).reshape(b, s, G, N)
    dt = jax.nn.softplus(dt_raw.astype(jnp.float32) + dt_bias.astype(jnp.float32)).reshape(b, s, G, R)
    a = -jnp.exp(a_log.astype(jnp.float32)).reshape(G, R)
    nc = s // CHUNK

    def to_chunks(t):
        return jnp.moveaxis(t.reshape(b, nc, CHUNK, *t.shape[2:]), 1, 0)

    xdt = to_chunks(xh * dt[..., None])
    adt = to_chunks(dt * a)
    bc_all = to_chunks(bm)
    cc_all = to_chunks(cm)
    causal = jnp.tril(jnp.ones((CHUNK, CHUNK), dtype=bool))[None, :, :, None, None]

    def step(state, inp):
        xc, ac, bc, cc = inp
        a_cum = jnp.cumsum(ac, axis=1)
        seg = a_cum[:, :, None] - a_cum[:, None, :]
        decay = jnp.exp(jnp.where(causal, seg, -jnp.inf))
        cb = jnp.einsum("blgn,bsgn->blsg", cc, bc)
        y_diag = jnp.einsum("blsg,blsgr,bsgrp->blgrp", cb, decay, xc)
        y_off = jnp.einsum("blgn,bgrpn->blgrp", cc, state) * jnp.exp(a_cum)[..., None]
        to_end = jnp.exp(a_cum[:, -1:] - a_cum)
        new_state = (state * jnp.exp(a_cum[:, -1])[..., None, None]
                     + jnp.einsum("blgn,blgr,blgrp->bgrpn", bc, to_end, xc))
        return new_state, y_diag + y_off

    state0 = jnp.zeros((b, G, R, P, N), jnp.float32)
    _, y = lax.scan(step, state0, (xdt, adt, bc_all, cc_all))
    y = jnp.moveaxis(y, 0, 1).reshape(b, s, G, R, P)
    y = y + d_skip.astype(jnp.float32).reshape(G, R)[..., None] * xh
    g = y.reshape(b, s, G, R * P) * jax.nn.silu(z.astype(jnp.float32).reshape(b, s, G, R * P))
    g = g * lax.rsqrt(jnp.mean(g * g, axis=-1, keepdims=True) + EPS)
    out = g.reshape(b, s, SSD_WIDTH) * norm_w.astype(jnp.float32)
    return out.astype(xbc.dtype)


def partial_rope(x, cos, sin):
    half = ROT_DIM // 2
    x1, x2, rest = x[..., :half], x[..., half:ROT_DIM], x[..., ROT_DIM:]
    c = cos[:, None, None, :]
    sn = sin[:, None, None, :]
    return jnp.concatenate([x1 * c - x2 * sn, x2 * c + x1 * sn, rest], axis=-1)


def diff_attention(q, k, v, q_norm_w, k_norm_w, lq1, lk1, lq2, lk2, subln_w, lambda_init):
    b, s, _ = q.shape
    H, HD = ATT_HEADS, ATT_HEAD_DIM
    q = rms_norm(q.reshape(b, s, H, 2, HD), q_norm_w)
    k = rms_norm(k.reshape(b, s, H, 2, HD), k_norm_w)
    v = v.reshape(b, s, H, 2 * HD)
    pos = jnp.arange(s, dtype=jnp.float32)
    inv_freq = ROPE_THETA ** (-jnp.arange(0, ROT_DIM, 2, dtype=jnp.float32) / ROT_DIM)
    ang = pos[:, None] * inv_freq[None, :]
    cos = jnp.cos(ang).astype(q.dtype)
    sin = jnp.sin(ang).astype(q.dtype)
    q = partial_rope(q, cos, sin)
    k = partial_rope(k, cos, sin)
    lam = (jnp.exp(jnp.sum(lq1.astype(jnp.float32) * lk1.astype(jnp.float32)))
           - jnp.exp(jnp.sum(lq2.astype(jnp.float32) * lk2.astype(jnp.float32)))
           + lambda_init)
    nb = s // Q_BLOCK
    qb = q.reshape(b, nb, Q_BLOCK, H, 2, HD).transpose(1, 0, 3, 4, 2, 5)
    kt = k.transpose(0, 2, 3, 1, 4)
    vt = v.transpose(0, 2, 1, 3)
    key_chunk = jnp.arange(s) // CHUNK
    scale = HD ** -0.5

    def attend(args):
        q_blk, blk = args
        q_chunk = (blk * Q_BLOCK + jnp.arange(Q_BLOCK)) // CHUNK
        mask = key_chunk[None, :] <= q_chunk[:, None]
        sc = jnp.einsum("bhjqd,bhjkd->bhjqk", q_blk, kt).astype(jnp.float32) * scale
        p = jax.nn.softmax(jnp.where(mask, sc, -jnp.inf), axis=-1)
        w = p[:, :, 0] - lam * p[:, :, 1]
        return jnp.einsum("bhqk,bhkv->bhqv", w.astype(vt.dtype), vt)

    o = lax.map(attend, (qb, jnp.arange(nb)))
    o = o.transpose(1, 0, 3, 2, 4).reshape(b, s, H, 2 * HD)
    o = rms_norm(o, subln_w) * (1.0 - lambda_init)
    return o.reshape(b, s, ATT_WIDTH)


def setup_inputs(seed: int = 0) -> dict:
    key = jax.random.key(seed)
    ks = jax.random.split(key, 24)
    f32 = jnp.float32
    nrm = lambda k, shape, sc: jax.random.normal(k, shape, f32) * sc
    x = jax.random.normal(ks[0], (BATCH, SEQ, D_MODEL), f32)
    dt0 = jnp.exp(jax.random.uniform(ks[6], (DEPTH, SSD_HEADS), f32, math.log(1e-3), math.log(1e-1)))
    dt_bias = dt0 + jnp.log(-jnp.expm1(-dt0))
    a_log = jnp.log(jax.random.uniform(ks[7], (DEPTH, SSD_HEADS), f32, 1.0, 16.0))
    return {
        "x": x,
        "norm1_w": 1.0 + nrm(ks[1], (DEPTH, D_MODEL), 0.02),
        "w_in": nrm(ks[2], (DEPTH, D_MODEL, IN_DIM), D_MODEL ** -0.5),
        "conv_w": nrm(ks[3], (DEPTH, CONV_WIDTH, CONV_DIM), CONV_WIDTH ** -0.5),
        "conv_b": nrm(ks[4], (DEPTH, CONV_DIM), 0.02),
        "dt_bias": dt_bias,
        "a_log": a_log,
        "d_skip": 1.0 + nrm(ks[8], (DEPTH, SSD_HEADS), 0.1),
        "ssd_norm_w": 1.0 + nrm(ks[9], (DEPTH, SSD_WIDTH), 0.02),
        "q_norm_w": 1.0 + nrm(ks[10], (DEPTH, ATT_HEAD_DIM), 0.02),
        "k_norm_w": 1.0 + nrm(ks[11], (DEPTH, ATT_HEAD_DIM), 0.02),
        "lambda_q1": nrm(ks[12], (DEPTH, ATT_HEAD_DIM), 0.1),
        "lambda_k1": nrm(ks[13], (DEPTH, ATT_HEAD_DIM), 0.1),
        "lambda_q2": nrm(ks[14], (DEPTH, ATT_HEAD_DIM), 0.1),
        "lambda_k2": nrm(ks[15], (DEPTH, ATT_HEAD_DIM), 0.1),
        "subln_w": 1.0 + nrm(ks[16], (DEPTH, 2 * ATT_HEAD_DIM), 0.02),
        "w_out": nrm(ks[17], (DEPTH, D_MIX, D_MODEL), D_MIX ** -0.5),
        "norm2_w": 1.0 + nrm(ks[18], (DEPTH, D_MODEL), 0.02),
        "w_gate": nrm(ks[19], (DEPTH, D_MODEL, D_FF), D_MODEL ** -0.5),
        "w_up": nrm(ks[20], (DEPTH, D_MODEL, D_FF), D_MODEL ** -0.5),
        "w_down": nrm(ks[21], (DEPTH, D_FF, D_MODEL), D_FF ** -0.5),
    }


def reference(x, norm1_w, w_in, conv_w, conv_b, dt_bias, a_log, d_skip, ssd_norm_w,
              q_norm_w, k_norm_w, lambda_q1, lambda_k1, lambda_q2, lambda_k2, subln_w,
              w_out, norm2_w, w_gate, w_up, w_down):
    for l in range(DEPTH):
        lambda_init = 0.8 - 0.6 * math.exp(-0.3 * l)
        h = rms_norm(x, norm1_w[l])
        proj = h @ w_in[l]
        z, xbc, dt_raw, q, k, v = jnp.split(proj, SPLITS, axis=-1)
        y_ssd = ssd_mixer(z, xbc, dt_raw, conv_w[l], conv_b[l], dt_bias[l], a_log[l],
                          d_skip[l], ssd_norm_w[l])
        y_att = diff_attention(q, k, v, q_norm_w[l], k_norm_w[l], lambda_q1[l], lambda_k1[l],
                               lambda_q2[l], lambda_k2[l], subln_w[l], lambda_init)
        x = x + jnp.concatenate([y_ssd, y_att], axis=-1) @ w_out[l]
        h = rms_norm(x, norm2_w[l])
        x = x + (jax.nn.silu(h @ w_gate[l]) * (h @ w_up[l])) @ w_down[l]
    return x
```

```python
import functools
import math

import jax
import jax.numpy as jnp
from jax import lax
from jax.experimental import pallas as pl
from jax.experimental.pallas import tpu as pltpu

F32 = jnp.float32
BF16 = jnp.bfloat16

D_MODEL = 1024
SSD_WIDTH = 1024
SSD_HEAD_DIM = 64
SSD_HEADS = 16
SSD_GROUPS = 2
SSD_STATE = 128
SSD_HEADS_PER_GROUP = SSD_HEADS // SSD_GROUPS
CONV_WIDTH = 4
CONV_DIM = SSD_WIDTH + 2 * SSD_GROUPS * SSD_STATE
ATT_WIDTH = 1024
ATT_HEAD_DIM = 64
ATT_HEADS = 8
ATT_V_DIM = 2 * ATT_HEAD_DIM
ROT_DIM = ATT_HEAD_DIM // 4
ROPE_THETA = 500000.0
MASK_CHUNK = 64
D_MIX = SSD_WIDTH + ATT_WIDTH
D_FF = 2816
EPS = 1e-6

LANES = 128
SUBLANES = 8
VMEM_LIMIT_BYTES = 56 * 1024 * 1024

PROJ_ROWS = 512
SSD_CHUNK = 128
ATT_BLOCK = 256
FFN_ROWS = 512
FFN_SPLITS = ((0, 1536), (1536, 2816))


def _resident(shape):
    nd = len(shape)
    return pl.BlockSpec(shape, lambda *_: (0,) * nd, pipeline_mode=pl.Buffered(1))


def _silu(x):
    return x / (1.0 + jnp.exp(-x))


def _softplus(x):
    return jnp.maximum(x, 0.0) + jnp.log(1.0 + jnp.exp(-jnp.abs(x)))


def _split3(a):
    hi = a.astype(BF16)
    r1 = a - hi.astype(F32)
    mid = r1.astype(BF16)
    lo = (r1 - mid.astype(F32)).astype(BF16)
    return hi, mid, lo


def _dot(a, b):
    return jnp.dot(a, b, preferred_element_type=F32)


def _dot_nt(a, b):
    return lax.dot_general(a, b, (((1,), (1,)), ((), ())), preferred_element_type=F32)


def _qk_post(t, cw, s1w, s2w, bd):
    outs = []
    for c in range(ATT_HEADS):
        tc = t[:, c * LANES:(c + 1) * LANES]
        ms = _dot((tc * tc).astype(BF16), bd)
        r = lax.rsqrt(ms + EPS)
        rot = (tc * cw + pltpu.roll(tc, ROT_DIM // 2, 1) * s1w
               + pltpu.roll(tc, LANES - ROT_DIM // 2, 1) * s2w)
        outs.append((rot * r).astype(BF16))
    return outs


def _inproj_body(x_ref, n1_ref, wz_ref, wxbc_ref, wdt_ref, wq_ref, wk_ref, wv_ref, dtb_ref,
                 qc_ref, qs1_ref, qs2_ref, kc_ref, ks1_ref, ks2_ref, bd_ref,
                 z_ref, xbc_ref, dt_ref, q_ref, k_ref, v_ref):
    x = x_ref[...]
    ms = jnp.mean(x * x, axis=-1, keepdims=True)
    h = (x * lax.rsqrt(ms + EPS) * n1_ref[...]).astype(BF16)

    z_ref[...] = _dot(h, wz_ref[...]).astype(BF16)
    xbc_ref[...] = _dot(h, wxbc_ref[...]).astype(BF16)

    lane = lax.broadcasted_iota(jnp.int32, (1, LANES), 1)
    dt = _softplus(_dot(h, wdt_ref[...]) + dtb_ref[...])
    dt_ref[...] = jnp.where(lane < SSD_HEADS, dt, 0.0)

    bd = bd_ref[...]
    qt = _qk_post(_dot(h, wq_ref[...]), qc_ref[...], qs1_ref[...], qs2_ref[...], bd)
    for c in range(ATT_HEADS):
        q_ref[c] = qt[c]
    kt = _qk_post(_dot(h, wk_ref[...]), kc_ref[...], ks1_ref[...], ks2_ref[...], bd)
    for c in range(ATT_HEADS):
        k_ref[c] = kt[c]
    v = _dot(h, wv_ref[...]).astype(BF16)
    for c in range(ATT_HEADS):
        v_ref[c] = v[:, c * LANES:(c + 1) * LANES]


def _in_projection(x2, n1, wz, wxbc, wdt, wq, wk, wv, dtb, qtabs, ktabs, bd, batch, seq):
    m = x2.shape[0]
    tm = min(PROJ_ROWS, seq)
    tiles_per_seq = seq // tm
    row = lambda i: (i, 0)
    tab = pl.BlockSpec((tm, LANES), lambda i: (i % tiles_per_seq, 0))
    head_out = pl.BlockSpec((None, ATT_HEADS, tm, LANES),
                            lambda i: (i // tiles_per_seq, 0, i % tiles_per_seq, 0))
    hshape = jax.ShapeDtypeStruct((batch, ATT_HEADS, seq, LANES), BF16)
    return pl.pallas_call(
        _inproj_body,
        grid=(m // tm,),
        in_specs=[
            pl.BlockSpec((tm, D_MODEL), row),
            _resident((1, D_MODEL)),
            _resident(wz.shape), _resident(wxbc.shape), _resident(wdt.shape),
            _resident(wq.shape), _resident(wk.shape), _resident(wv.shape),
            _resident((1, LANES)),
            tab, tab, tab, tab, tab, tab,
            _resident((LANES, LANES)),
        ],
        out_specs=[
            pl.BlockSpec((tm, SSD_WIDTH), row),
            pl.BlockSpec((tm, CONV_DIM), row),
            pl.BlockSpec((tm, LANES), row),
            head_out, head_out, head_out,
        ],
        out_shape=[
            jax.ShapeDtypeStruct((m, SSD_WIDTH), BF16),
            jax.ShapeDtypeStruct((m, CONV_DIM), BF16),
            jax.ShapeDtypeStruct((m, LANES), F32),
            hshape, hshape, hshape,
        ],
        compiler_params=pltpu.CompilerParams(
            dimension_semantics=("arbitrary",), vmem_limit_bytes=VMEM_LIMIT_BYTES),
        name="in_projection",
    )(x2, n1, wz, wxbc, wdt, wq, wk, wv, dtb, *qtabs, *ktabs, bd)


def _ssd_body(xbc_ref, z_ref, dt_ref, convw_ref, convb_ref, alog_ref, dskip_ref, normw_ref,
              y_ref, convbuf, state):
    L = xbc_ref.shape[0]
    N = SSD_STATE
    GW = SSD_WIDTH // SSD_GROUPS
    c = pl.program_id(1)

    @pl.when(c == 0)
    def _init():
        convbuf[0:SUBLANES, :] = jnp.zeros((SUBLANES, CONV_DIM), F32)
        state[...] = jnp.zeros_like(state)

    convbuf[SUBLANES:SUBLANES + L, :] = xbc_ref[...].astype(F32)
    acc = jnp.broadcast_to(convb_ref[...], (L, CONV_DIM))
    for k in range(CONV_WIDTH):
        off = SUBLANES - (CONV_WIDTH - 1) + k
        acc = acc + convw_ref[k:k + 1, :] * convbuf[off:off + L, :]
    convbuf[0:SUBLANES, :] = convbuf[L:L + SUBLANES, :]
    xc = _silu(acc)

    dt = dt_ref[...]
    a = -jnp.exp(alog_ref[...])
    adt = dt * a
    rows = lax.broadcasted_iota(jnp.int32, (L, L), 0)
    cols = lax.broadcasted_iota(jnp.int32, (L, L), 1)
    causal = rows >= cols
    tri = jnp.where(causal, 1.0, 0.0).astype(BF16)
    hi, mid, lo = _split3(adt)
    a_cum = _dot(tri, hi) + _dot(tri, mid) + _dot(tri, lo)
    a_last = a_cum[L - 1:L, :]
    wgt = dt * jnp.exp(a_last - a_cum)
    e_last = jnp.exp(a_last)
    a_cum_t = a_cum.T
    dt_t = dt.T

    lane = lax.broadcasted_iota(jnp.int32, (1, LANES), 1)
    lo_half = lane < SSD_HEAD_DIM

    y_tiles = []
    for g in range(SSD_GROUPS):
        bg = xc[:, SSD_WIDTH + g * N:SSD_WIDTH + (g + 1) * N]
        cg = xc[:, SSD_WIDTH + SSD_GROUPS * N + g * N:SSD_WIDTH + SSD_GROUPS * N + (g + 1) * N]
        bg16 = bg.astype(BF16)
        cg16 = cg.astype(BF16)
        cb = _dot_nt(cg16, bg16)
        st = state[g]
        st16 = st.astype(BF16)
        bg_t16 = bg.T.astype(BF16)
        xw_parts = []
        dec_parts = []
        for pr in range(SSD_HEADS_PER_GROUP // 2):
            h0 = g * SSD_HEADS_PER_GROUP + 2 * pr
            lhs_rows = []
            for hh in (h0, h0 + 1):
                col = a_cum[:, hh:hh + 1]
                row = a_cum_t[hh:hh + 1, :]
                dec = jnp.exp(jnp.where(causal, col - row, -jnp.inf))
                mh = cb * dec * dt_t[hh:hh + 1, :]
                ce = cg * jnp.exp(col)
                lhs_rows.append(jnp.concatenate([mh.astype(BF16), ce.astype(BF16)], axis=1))
            lhs = jnp.concatenate(lhs_rows, axis=0)
            xs_pair = xc[:, h0 * SSD_HEAD_DIM:(h0 + 2) * SSD_HEAD_DIM]
            rhs = jnp.concatenate(
                [xs_pair.astype(BF16), st16[:, pr * LANES:(pr + 1) * LANES]], axis=0)
            yy = _dot(lhs, rhs)
            y_tiles.append(jnp.where(lo_half, yy[:L], yy[L:]))
            wb = jnp.where(lo_half, wgt[:, h0:h0 + 1], wgt[:, h0 + 1:h0 + 2])
            xw_parts.append((xs_pair * wb).astype(BF16))
            dec_parts.append(jnp.where(lo_half, e_last[:, h0:h0 + 1], e_last[:, h0 + 1:h0 + 2]))
        xw = jnp.concatenate(xw_parts, axis=1)
        state[g] = st * jnp.concatenate(dec_parts, axis=1) + _dot(bg_t16, xw)

    y = jnp.concatenate(y_tiles, axis=1) + dskip_ref[...] * xc[:, :SSD_WIDTH]
    gz = y * _silu(z_ref[...].astype(F32))
    outs = []
    for g in range(SSD_GROUPS):
        gg = gz[:, g * GW:(g + 1) * GW]
        ms = jnp.mean(gg * gg, axis=-1, keepdims=True)
        outs.append(gg * lax.rsqrt(ms + EPS))
    y_ref[...] = (jnp.concatenate(outs, axis=1) * normw_ref[...]).astype(BF16)


def _ssd_mixer(xbc, z, dt, convw, convb, alog, dskip, normw, batch, seq):
    m = xbc.shape[0]
    L = min(SSD_CHUNK, seq)
    nc = seq // L
    row = lambda b, c: (b * nc + c, 0)
    return pl.pallas_call(
        _ssd_body,
        grid=(batch, nc),
        in_specs=[
            pl.BlockSpec((L, CONV_DIM), row),
            pl.BlockSpec((L, SSD_WIDTH), row),
            pl.BlockSpec((L, LANES), row),
            _resident((CONV_WIDTH, CONV_DIM)),
            _resident((1, CONV_DIM)),
            _resident((1, LANES)),
            _resident((1, SSD_WIDTH)),
            _resident((1, SSD_WIDTH)),
        ],
        out_specs=pl.BlockSpec((L, SSD_WIDTH), row),
        out_shape=jax.ShapeDtypeStruct((m, SSD_WIDTH), BF16),
        scratch_shapes=[
            pltpu.VMEM((L + SUBLANES, CONV_DIM), F32),
            pltpu.VMEM((SSD_GROUPS, SSD_STATE, SSD_WIDTH // SSD_GROUPS), F32),
        ],
        compiler_params=pltpu.CompilerParams(
            dimension_semantics=("arbitrary", "arbitrary"), vmem_limit_bytes=VMEM_LIMIT_BYTES),
        name="ssd_mixer",
    )(xbc, z, dt, convw, convb, alog, dskip, normw)


def _attn_body(q_ref, k_ref, v_ref, lq1_ref, lk1_ref, lq2_ref, lk2_ref, subw_ref, o_ref,
               m_sc, l_sc, acc_sc, *, lambda_init):
    S = q_ref.shape[0]
    T = min(ATT_BLOCK, S)
    nq = S // T
    lam = (jnp.exp(jnp.sum(lq1_ref[...] * lk1_ref[...], axis=-1, keepdims=True))
           - jnp.exp(jnp.sum(lq2_ref[...] * lk2_ref[...], axis=-1, keepdims=True))
           + lambda_init)
    lane = lax.broadcasted_iota(jnp.int32, (1, LANES), 1)
    lo_half = lane < ATT_HEAD_DIM
    r_chunk = (lax.broadcasted_iota(jnp.int32, (2 * T, T), 0) % T) // MASK_CHUNK
    c_chunk = lax.broadcasted_iota(jnp.int32, (2 * T, T), 1) // MASK_CHUNK
    diag_mask = c_chunk <= r_chunk

    def step(q2, kv_start, mask):
        kb = k_ref[pl.ds(kv_start, T), :]
        vb = v_ref[pl.ds(kv_start, T), :]
        s = _dot_nt(q2, kb)
        if mask is not None:
            s = jnp.where(mask, s, -jnp.inf)
        m_prev = m_sc[...]
        m_new = jnp.maximum(m_prev, jnp.max(s, axis=-1, keepdims=True))
        alpha = jnp.exp(m_prev - m_new)
        p = jnp.exp(s - m_new)
        l_sc[...] = alpha * l_sc[...] + jnp.sum(p, axis=-1, keepdims=True)
        acc_sc[...] = alpha * acc_sc[...] + _dot(p.astype(BF16), vb)
        m_sc[...] = m_new

    def q_block(qi, carry):
        q_start = pl.multiple_of(qi * T, T)
        qb = q_ref[pl.ds(q_start, T), :]
        zero = jnp.zeros_like(qb)
        q2 = jnp.concatenate([jnp.where(lo_half, qb, zero), jnp.where(lo_half, zero, qb)], axis=0)
        m_sc[...] = jnp.full(m_sc.shape, -jnp.inf, F32)
        l_sc[...] = jnp.zeros(l_sc.shape, F32)
        acc_sc[...] = jnp.zeros(acc_sc.shape, F32)

        def kv_block(j, c2):
            step(q2, pl.multiple_of(j * T, T), None)
            return c2

        lax.fori_loop(0, qi, kv_block, 0)
        step(q2, q_start, diag_mask)

        o2 = acc_sc[...] / l_sc[...]
        o = o2[:T] - lam * o2[T:]
        ms = jnp.mean(o * o, axis=-1, keepdims=True)
        o = o * lax.rsqrt(ms + EPS) * subw_ref[...] * (1.0 - lambda_init)
        o_ref[pl.ds(q_start, T), :] = o.astype(BF16)
        return carry

    lax.fori_loop(0, nq, q_block, 0)


def _diff_attention(q, k, v, lq1, lk1, lq2, lk2, subw, lambda_init):
    batch, heads, seq, _ = q.shape
    T = min(ATT_BLOCK, seq)
    blk = pl.BlockSpec((None, None, seq, LANES), lambda b, h: (b, h, 0, 0))
    vec = _resident((1, ATT_HEAD_DIM))
    return pl.pallas_call(
        functools.partial(_attn_body, lambda_init=lambda_init),
        grid=(batch, heads),
        in_specs=[blk, blk, blk, vec, vec, vec, vec, _resident((1, LANES))],
        out_specs=blk,
        out_shape=jax.ShapeDtypeStruct(q.shape, BF16),
        scratch_shapes=[
            pltpu.VMEM((2 * T, 1), F32),
            pltpu.VMEM((2 * T, 1), F32),
            pltpu.VMEM((2 * T, LANES), F32),
        ],
        compiler_params=pltpu.CompilerParams(
            dimension_semantics=("arbitrary", "arbitrary"), vmem_limit_bytes=VMEM_LIMIT_BYTES),
        name="diff_attention",
    )(q, k, v, lq1, lk1, lq2, lk2, subw)


def _outffn_body(x_ref, ys_ref, ya_ref, wout_ref, n2_ref, wg_ref, wu_ref, wd_ref, o_ref):
    ycat = jnp.concatenate([ys_ref[...]] + [ya_ref[c] for c in range(ATT_HEADS)], axis=1)
    x1 = x_ref[...] + _dot(ycat, wout_ref[...])
    ms = jnp.mean(x1 * x1, axis=-1, keepdims=True)
    h = (x1 * lax.rsqrt(ms + EPS) * n2_ref[...]).astype(BF16)
    out = x1
    for lo, hi in FFN_SPLITS:
        gate = _dot(h, wg_ref[:, lo:hi])
        up = _dot(h, wu_ref[:, lo:hi])
        act = (_silu(gate) * up).astype(BF16)
        out = out + _dot(act, wd_ref[lo:hi, :])
    o_ref[...] = out


def _out_ffn(x2, ys, ya, wout, n2, wg, wu, wd, batch, seq):
    m = x2.shape[0]
    tm = min(FFN_ROWS, seq)
    tiles_per_seq = seq // tm
    row = lambda i: (i, 0)
    return pl.pallas_call(
        _outffn_body,
        grid=(m // tm,),
        in_specs=[
            pl.BlockSpec((tm, D_MODEL), row),
            pl.BlockSpec((tm, SSD_WIDTH), row),
            pl.BlockSpec((None, ATT_HEADS, tm, LANES),
                         lambda i: (i // tiles_per_seq, 0, i % tiles_per_seq, 0)),
            _resident(wout.shape), _resident((1, D_MODEL)),
            _resident(wg.shape), _resident(wu.shape), _resident(wd.shape),
        ],
        out_specs=pl.BlockSpec((tm, D_MODEL), row),
        out_shape=jax.ShapeDtypeStruct((m, D_MODEL), F32),
        compiler_params=pltpu.CompilerParams(
            dimension_semantics=("arbitrary",), vmem_limit_bytes=VMEM_LIMIT_BYTES),
        name="out_ffn",
    )(x2, ys, ya, wout, n2, wg, wu, wd)


def _rope_tables(seq, norm_w, scale):
    half = ROT_DIM // 2
    pos = jnp.arange(seq, dtype=F32)
    inv_freq = ROPE_THETA ** (-jnp.arange(0, ROT_DIM, 2, dtype=F32) / ROT_DIM)
    ang = pos[:, None] * inv_freq[None, :]
    cos, sin = jnp.cos(ang), jnp.sin(ang)
    rest = ATT_HEAD_DIM - ROT_DIM
    c64 = jnp.concatenate([cos, cos, jnp.ones((seq, rest), F32)], axis=1)
    s1 = jnp.concatenate([jnp.zeros((seq, half), F32), sin, jnp.zeros((seq, rest), F32)], axis=1)
    s2 = jnp.concatenate([-sin, jnp.zeros((seq, ATT_HEAD_DIM - half), F32)], axis=1)
    w = norm_w.astype(F32) * scale
    cw = c64 * w[None, :]
    s1w = s1 * jnp.roll(w, half)[None, :]
    s2w = s2 * jnp.roll(w, -half)[None, :]
    return tuple(jnp.tile(t, (1, LANES // ATT_HEAD_DIM)) for t in (cw, s1w, s2w))


def _pad_lanes(v, width=LANES):
    v = v.reshape(1, -1).astype(F32)
    return jnp.pad(v, ((0, 0), (0, width - v.shape[1])))


def _layer(x, l, norm1_w, w_in, conv_w, conv_b, dt_bias, a_log, d_skip, ssd_norm_w,
           q_norm_w, k_norm_w, lambda_q1, lambda_k1, lambda_q2, lambda_k2, subln_w,
           w_out, norm2_w, w_gate, w_up, w_down):
    batch, seq, _ = x.shape
    lambda_init = 0.8 - 0.6 * math.exp(-0.3 * l)
    x2 = x.reshape(batch * seq, D_MODEL)

    o_xbc = SSD_WIDTH
    o_dt = o_xbc + CONV_DIM
    o_q = o_dt + SSD_HEADS
    o_k = o_q + ATT_WIDTH
    o_v = o_k + ATT_WIDTH
    wb = w_in.astype(BF16)
    wz, wxbc = wb[:, :o_xbc], wb[:, o_xbc:o_dt]
    wdt = jnp.pad(wb[:, o_dt:o_q], ((0, 0), (0, LANES - SSD_HEADS)))
    wq, wk, wv = wb[:, o_q:o_k], wb[:, o_k:o_v], wb[:, o_v:]

    qtabs = _rope_tables(seq, q_norm_w, ATT_HEAD_DIM ** -0.5)
    ktabs = _rope_tables(seq, k_norm_w, 1.0)
    blk = jnp.arange(LANES) // ATT_HEAD_DIM
    bd = jnp.where(blk[:, None] == blk[None, :], 1.0 / ATT_HEAD_DIM, 0.0).astype(BF16)

    z, xbc, dt, q, k, v = _in_projection(
        x2, norm1_w.reshape(1, -1), wz, wxbc, wdt, wq, wk, wv, _pad_lanes(dt_bias),
        qtabs, ktabs, bd, batch, seq)

    y_ssd = _ssd_mixer(
        xbc, z, dt, conv_w.astype(F32), conv_b.reshape(1, -1).astype(F32), _pad_lanes(a_log),
        jnp.repeat(d_skip.astype(F32), SSD_HEAD_DIM).reshape(1, -1),
        ssd_norm_w.reshape(1, -1).astype(F32), batch, seq)

    vec = lambda t: t.reshape(1, -1).astype(F32)
    y_att = _diff_attention(q, k, v, vec(lambda_q1), vec(lambda_k1), vec(lambda_q2),
                            vec(lambda_k2), vec(subln_w), lambda_init)

    out = _out_ffn(x2, y_ssd, y_att, w_out.astype(BF16), norm2_w.reshape(1, -1).astype(F32),
                   w_gate.astype(BF16), w_up.astype(BF16), w_down.astype(BF16), batch, seq)
    return out.reshape(batch, seq, D_MODEL)


def kernel(x, norm1_w, w_in, conv_w, conv_b, dt_bias, a_log, d_skip, ssd_norm_w, q_norm_w, k_norm_w, lambda_q1, lambda_k1, lambda_q2, lambda_k2, subln_w, w_out, norm2_w, w_gate, w_up, w_down):
    depth = w_in.shape[0]
    for l in range(depth):
        x = _layer(x, l, norm1_w[l], w_in[l], conv_w[l], conv_b[l], dt_bias[l], a_log[l],
                   d_skip[l], ssd_norm_w[l], q_norm_w[l], k_norm_w[l], lambda_q1[l], lambda_k1[l],
                   lambda_q2[l], lambda_k2[l], subln_w[l], w_out[l], norm2_w[l], w_gate[l],
                   w_up[l], w_down[l])
    return x
```

```python
import functools
import math

import jax
import jax.numpy as jnp
from jax import lax
from jax.experimental import pallas as pl
from jax.experimental.pallas import tpu as pltpu

F32 = jnp.float32
BF16 = jnp.bfloat16

D_MODEL = 1024
SSD_WIDTH = 1024
SSD_HEAD_DIM = 64
SSD_HEADS = 16
SSD_GROUPS = 2
SSD_STATE = 128
SSD_HEADS_PER_GROUP = SSD_HEADS // SSD_GROUPS
CONV_WIDTH = 4
CONV_DIM = SSD_WIDTH + 2 * SSD_GROUPS * SSD_STATE
ATT_WIDTH = 1024
ATT_HEAD_DIM = 64
ATT_HEADS = 8
ATT_V_DIM = 2 * ATT_HEAD_DIM
ROT_DIM = ATT_HEAD_DIM // 4
ROPE_THETA = 500000.0
MASK_CHUNK = 64
D_MIX = SSD_WIDTH + ATT_WIDTH
D_FF = 2816
EPS = 1e-6

LANES = 128
SUBLANES = 8
VMEM_LIMIT_BYTES = 56 * 1024 * 1024

PROJ_ROWS = 512
SSD_CHUNK = 128
ATT_BLOCK = 256
FFN_ROWS = 512
FFN_SPLITS = ((0, 1536), (1536, 2816))


def _resident(shape):
    nd = len(shape)
    return pl.BlockSpec(shape, lambda *_: (0,) * nd, pipeline_mode=pl.Buffered(1))


def _silu(x):
    return x / (1.0 + jnp.exp(-x))


def _softplus(x):
    return jnp.maximum(x, 0.0) + jnp.log(1.0 + jnp.exp(-jnp.abs(x)))


def _split3(a):
    hi = a.astype(BF16)
    r1 = a - hi.astype(F32)
    mid = r1.astype(BF16)
    lo = (r1 - mid.astype(F32)).astype(BF16)
    return hi, mid, lo


def _dot(a, b):
    return jnp.dot(a, b, preferred_element_type=F32)


def _dot_nt(a, b):
    return lax.dot_general(a, b, (((1,), (1,)), ((), ())), preferred_element_type=F32)


def _qk_post(t, cw, s1w, s2w, bd):
    outs = []
    for c in range(ATT_HEADS):
        tc = t[:, c * LANES:(c + 1) * LANES]
        ms = _dot((tc * tc).astype(BF16), bd)
        r = lax.rsqrt(ms + EPS)
        rot = (tc * cw + pltpu.roll(tc, ROT_DIM // 2, 1) * s1w
               + pltpu.roll(tc, LANES - ROT_DIM // 2, 1) * s2w)
        outs.append((rot * r).astype(BF16))
    return outs


def _inproj_body(x_ref, n1_ref, wz_ref, wxbc_ref, wdt_ref, wq_ref, wk_ref, wv_ref, dtb_ref,
                 qc_ref, qs1_ref, qs2_ref, kc_ref, ks1_ref, ks2_ref, bd_ref,
                 z_ref, xbc_ref, dt_ref, q_ref, k_ref, v_ref):
    x = x_ref[...]
    ms = jnp.mean(x * x, axis=-1, keepdims=True)
    h = (x * lax.rsqrt(ms + EPS) * n1_ref[...]).astype(BF16)

    z_ref[...] = _dot(h, wz_ref[...]).astype(BF16)
    xbc_ref[...] = _dot(h, wxbc_ref[...]).astype(BF16)

    lane = lax.broadcasted_iota(jnp.int32, (1, LANES), 1)
    dt = _softplus(_dot(h, wdt_ref[...]) + dtb_ref[...])
    dt_ref[...] = jnp.where(lane < SSD_HEADS, dt, 0.0)

    bd = bd_ref[...]
    qt = _qk_post(_dot(h, wq_ref[...]), qc_ref[...], qs1_ref[...], qs2_ref[...], bd)
    for c in range(ATT_HEADS):
        q_ref[c] = qt[c]
    kt = _qk_post(_dot(h, wk_ref[...]), kc_ref[...], ks1_ref[...], ks2_ref[...], bd)
    for c in range(ATT_HEADS):
        k_ref[c] = kt[c]
    v = _dot(h, wv_ref[...]).astype(BF16)
    for c in range(ATT_HEADS):
        v_ref[c] = v[:, c * LANES:(c + 1) * LANES]


def _in_projection(x2, n1, wz, wxbc, wdt, wq, wk, wv, dtb, qtabs, ktabs, bd, batch, seq):
    m = x2.shape[0]
    tm = min(PROJ_ROWS, seq)
    tiles_per_seq = seq // tm
    row = lambda i: (i, 0)
    tab = pl.BlockSpec((tm, LANES), lambda i: (i % tiles_per_seq, 0))
    head_out = pl.BlockSpec((None, ATT_HEADS, tm, LANES),
                            lambda i: (i // tiles_per_seq, 0, i % tiles_per_seq, 0))
    hshape = jax.ShapeDtypeStruct((batch, ATT_HEADS, seq, LANES), BF16)
    return pl.pallas_call(
        _inproj_body,
        grid=(m // tm,),
        in_specs=[
            pl.BlockSpec((tm, D_MODEL), row),
            _resident((1, D_MODEL)),
            _resident(wz.shape), _resident(wxbc.shape), _resident(wdt.shape),
            _resident(wq.shape), _resident(wk.shape), _resident(wv.shape),
            _resident((1, LANES)),
            tab, tab, tab, tab, tab, tab,
            _resident((LANES, LANES)),
        ],
        out_specs=[
            pl.BlockSpec((tm, SSD_WIDTH), row),
            pl.BlockSpec((tm, CONV_DIM), row),
            pl.BlockSpec((tm, LANES), row),
            head_out, head_out, head_out,
        ],
        out_shape=[
            jax.ShapeDtypeStruct((m, SSD_WIDTH), BF16),
            jax.ShapeDtypeStruct((m, CONV_DIM), BF16),
            jax.ShapeDtypeStruct((m, LANES), F32),
            hshape, hshape, hshape,
        ],
        compiler_params=pltpu.CompilerParams(
            dimension_semantics=("arbitrary",), vmem_limit_bytes=VMEM_LIMIT_BYTES),
        name="in_projection",
    )(x2, n1, wz, wxbc, wdt, wq, wk, wv, dtb, *qtabs, *ktabs, bd)


def _ssd_body(xbc_ref, z_ref, dt_ref, convw_ref, convb_ref, alog_ref, dskip_ref, normw_ref,
              y_ref, convbuf, state):
    L = xbc_ref.shape[0]
    N = SSD_STATE
    GW = SSD_WIDTH // SSD_GROUPS
    c = pl.program_id(1)

    @pl.when(c == 0)
    def _init():
        convbuf[0:SUBLANES, :] = jnp.zeros((SUBLANES, CONV_DIM), F32)
        state[...] = jnp.zeros_like(state)

    convbuf[SUBLANES:SUBLANES + L, :] = xbc_ref[...].astype(F32)
    acc = jnp.broadcast_to(convb_ref[...], (L, CONV_DIM))
    for k in range(CONV_WIDTH):
        off = SUBLANES - (CONV_WIDTH - 1) + k
        acc = acc + convw_ref[k:k + 1, :] * convbuf[off:off + L, :]
    convbuf[0:SUBLANES, :] = convbuf[L:L + SUBLANES, :]
    xc = _silu(acc)

    dt = dt_ref[...]
    a = -jnp.exp(alog_ref[...])
    adt = dt * a
    rows = lax.broadcasted_iota(jnp.int32, (L, L), 0)
    cols = lax.broadcasted_iota(jnp.int32, (L, L), 1)
    causal = rows >= cols
    tri = jnp.where(causal, 1.0, 0.0).astype(BF16)
    hi, mid, lo = _split3(adt)
    a_cum = _dot(tri, hi) + _dot(tri, mid) + _dot(tri, lo)
    a_last = a_cum[L - 1:L, :]
    wgt = dt * jnp.exp(a_last - a_cum)
    e_last = jnp.exp(a_last)
    a_cum_t = a_cum.T
    dt_t = dt.T

    lane = lax.broadcasted_iota(jnp.int32, (1, LANES), 1)
    lo_half = lane < SSD_HEAD_DIM

    y_tiles = []
    for g in range(SSD_GROUPS):
        bg = xc[:, SSD_WIDTH + g * N:SSD_WIDTH + (g + 1) * N]
        cg = xc[:, SSD_WIDTH + SSD_GROUPS * N + g * N:SSD_WIDTH + SSD_GROUPS * N + (g + 1) * N]
        bg16 = bg.astype(BF16)
        cg16 = cg.astype(BF16)
        cb = _dot_nt(cg16, bg16)
        st = state[g]
        st16 = st.astype(BF16)
        bg_t16 = bg.T.astype(BF16)
        xw_parts = []
        dec_parts = []
        for pr in range(SSD_HEADS_PER_GROUP // 2):
            h0 = g * SSD_HEADS_PER_GROUP + 2 * pr
            lhs_rows = []
            for hh in (h0, h0 + 1):
                col = a_cum[:, hh:hh + 1]
                row = a_cum_t[hh:hh + 1, :]
                dec = jnp.exp(jnp.where(causal, col - row, -jnp.inf))
                mh = cb * dec * dt_t[hh:hh + 1, :]
                ce = cg * jnp.exp(col)
                lhs_rows.append(jnp.concatenate([mh.astype(BF16), ce.astype(BF16)], axis=1))
            lhs = jnp.concatenate(lhs_rows, axis=0)
            xs_pair = xc[:, h0 * SSD_HEAD_DIM:(h0 + 2) * SSD_HEAD_DIM]
            rhs = jnp.concatenate(
                [xs_pair.astype(BF16), st16[:, pr * LANES:(pr + 1) * LANES]], axis=0)
            yy = _dot(lhs, rhs)
            y_tiles.append(jnp.where(lo_half, yy[:L], yy[L:]))
            wb = jnp.where(lo_half, wgt[:, h0:h0 + 1], wgt[:, h0 + 1:h0 + 2])
            xw_parts.append((xs_pair * wb).astype(BF16))
            dec_parts.append(jnp.where(lo_half, e_last[:, h0:h0 + 1], e_last[:, h0 + 1:h0 + 2]))
        xw = jnp.concatenate(xw_parts, axis=1)
        state[g] = st * jnp.concatenate(dec_parts, axis=1) + _dot(bg_t16, xw)

    y = jnp.concatenate(y_tiles, axis=1) + dskip_ref[...] * xc[:, :SSD_WIDTH]
    gz = y * _silu(z_ref[...].astype(F32))
    outs = []
    for g in range(SSD_GROUPS):
        gg = gz[:, g * GW:(g + 1) * GW]
        ms = jnp.mean(gg * gg, axis=-1, keepdims=True)
        outs.append(gg * lax.rsqrt(ms + EPS))
    y_ref[...] = (jnp.concatenate(outs, axis=1) * normw_ref[...]).astype(BF16)


def _ssd_mixer(xbc, z, dt, convw, convb, alog, dskip, normw, batch, seq):
    m = xbc.shape[0]
    L = min(SSD_CHUNK, seq)
    nc = seq // L
    row = lambda b, c: (b * nc + c, 0)
    return pl.pallas_call(
        _ssd_body,
        grid=(batch, nc),
        in_specs=[
            pl.BlockSpec((L, CONV_DIM), row),
            pl.BlockSpec((L, SSD_WIDTH), row),
            pl.BlockSpec((L, LANES), row),
            _resident((CONV_WIDTH, CONV_DIM)),
            _resident((1, CONV_DIM)),
            _resident((1, LANES)),
            _resident((1, SSD_WIDTH)),
            _resident((1, SSD_WIDTH)),
        ],
        out_specs=pl.BlockSpec((L, SSD_WIDTH), row),
        out_shape=jax.ShapeDtypeStruct((m, SSD_WIDTH), BF16),
        scratch_shapes=[
            pltpu.VMEM((L + SUBLANES, CONV_DIM), F32),
            pltpu.VMEM((SSD_GROUPS, SSD_STATE, SSD_WIDTH // SSD_GROUPS), F32),
        ],
        compiler_params=pltpu.CompilerParams(
            dimension_semantics=("arbitrary", "arbitrary"), vmem_limit_bytes=VMEM_LIMIT_BYTES),
        name="ssd_mixer",
    )(xbc, z, dt, convw, convb, alog, dskip, normw)


ATT_SCORE_BOUND_COEF = ATT_HEAD_DIM * ATT_HEAD_DIM ** -0.5
ATT_FIXED_SHIFT_MAX = 40.0


def _attn_lambda(lq1_ref, lk1_ref, lq2_ref, lk2_ref, lambda_init):
    return (jnp.exp(jnp.sum(lq1_ref[...] * lk1_ref[...], axis=-1, keepdims=True))
            - jnp.exp(jnp.sum(lq2_ref[...] * lk2_ref[...], axis=-1, keepdims=True))
            + lambda_init)


def _stack_subheads(qb):
    lane = lax.broadcasted_iota(jnp.int32, (1, LANES), 1)
    lo_half = lane < ATT_HEAD_DIM
    zero = jnp.zeros_like(qb)
    return jnp.concatenate([jnp.where(lo_half, qb, zero), jnp.where(lo_half, zero, qb)], axis=0)


def _diag_mask(T):
    r_chunk = (lax.broadcasted_iota(jnp.int32, (2 * T, T), 0) % T) // MASK_CHUNK
    c_chunk = lax.broadcasted_iota(jnp.int32, (2 * T, T), 1) // MASK_CHUNK
    return c_chunk <= r_chunk


def _attn_finish(o2, lam, subw, lambda_init, T):
    o = o2[:T] - lam * o2[T:]
    ms = jnp.mean(o * o, axis=-1, keepdims=True)
    return (o * lax.rsqrt(ms + EPS) * subw * (1.0 - lambda_init)).astype(BF16)


def _attn_fixed_shift_body(q_ref, k_ref, v_ref, lq1_ref, lk1_ref, lq2_ref, lk2_ref, subw_ref,
                           shift_ref, o_ref, v1_sc, *, lambda_init):
    S = q_ref.shape[0]
    T = min(ATT_BLOCK, S)
    nq = S // T
    lam = _attn_lambda(lq1_ref, lk1_ref, lq2_ref, lk2_ref, lambda_init)
    shift = shift_ref[...]
    mask = _diag_mask(T)
    v1_sc[:, :LANES] = v_ref[...]
    v1_sc[:, LANES:] = jnp.ones((S, LANES), BF16)
    for qi in range(nq):
        q2 = _stack_subheads(q_ref[qi * T:(qi + 1) * T, :])
        acc = jnp.zeros((2 * T, 2 * LANES), F32)
        for j in range(qi + 1):
            s = _dot_nt(q2, k_ref[j * T:(j + 1) * T, :])
            p = jnp.exp(s - shift)
            if j == qi:
                p = jnp.where(mask, p, 0.0)
            acc = acc + _dot(p.astype(BF16), v1_sc[j * T:(j + 1) * T, :])
        o2 = acc[:, :LANES] / acc[:, LANES:]
        o_ref[qi * T:(qi + 1) * T, :] = _attn_finish(o2, lam, subw_ref[...], lambda_init, T)


def _attn_online_body(q_ref, k_ref, v_ref, lq1_ref, lk1_ref, lq2_ref, lk2_ref, subw_ref, o_ref,
                      m_sc, l_sc, acc_sc, *, lambda_init):
    S = q_ref.shape[0]
    T = min(ATT_BLOCK, S)
    nq = S // T
    lam = _attn_lambda(lq1_ref, lk1_ref, lq2_ref, lk2_ref, lambda_init)
    mask = _diag_mask(T)

    def step(q2, kv_start, masked):
        kb = k_ref[pl.ds(kv_start, T), :]
        vb = v_ref[pl.ds(kv_start, T), :]
        s = _dot_nt(q2, kb)
        if masked:
            s = jnp.where(mask, s, -jnp.inf)
        m_prev = m_sc[...]
        m_new = jnp.maximum(m_prev, jnp.max(s, axis=-1, keepdims=True))
        alpha = jnp.exp(m_prev - m_new)
        p = jnp.exp(s - m_new)
        l_sc[...] = alpha * l_sc[...] + jnp.sum(p, axis=-1, keepdims=True)
        acc_sc[...] = alpha * acc_sc[...] + _dot(p.astype(BF16), vb)
        m_sc[...] = m_new

    def q_block(qi, carry):
        q_start = pl.multiple_of(qi * T, T)
        q2 = _stack_subheads(q_ref[pl.ds(q_start, T), :])
        m_sc[...] = jnp.full(m_sc.shape, -jnp.inf, F32)
        l_sc[...] = jnp.zeros(l_sc.shape, F32)
        acc_sc[...] = jnp.zeros(acc_sc.shape, F32)

        def kv_block(j, c2):
            step(q2, pl.multiple_of(j * T, T), False)
            return c2

        lax.fori_loop(0, qi, kv_block, 0)
        step(q2, q_start, True)
        o2 = acc_sc[...] / l_sc[...]
        o_ref[pl.ds(q_start, T), :] = _attn_finish(o2, lam, subw_ref[...], lambda_init, T)
        return carry

    lax.fori_loop(0, nq, q_block, 0)


def _diff_attention(q, k, v, lq1, lk1, lq2, lk2, subw, q_norm_w, k_norm_w, lambda_init):
    batch, heads, seq, _ = q.shape
    T = min(ATT_BLOCK, seq)
    blk = pl.BlockSpec((None, None, seq, LANES), lambda b, h: (b, h, 0, 0))
    vec = _resident((1, ATT_HEAD_DIM))
    params = pltpu.CompilerParams(
        dimension_semantics=("arbitrary", "arbitrary"), vmem_limit_bytes=VMEM_LIMIT_BYTES)
    out_shape = jax.ShapeDtypeStruct(q.shape, BF16)
    bound = ATT_SCORE_BOUND_COEF * jnp.max(jnp.abs(q_norm_w)) * jnp.max(jnp.abs(k_norm_w))
    bound = bound.astype(F32).reshape(1, 1)

    def fixed_shift(args):
        return pl.pallas_call(
            functools.partial(_attn_fixed_shift_body, lambda_init=lambda_init),
            grid=(batch, heads),
            in_specs=[blk, blk, blk, vec, vec, vec, vec, _resident((1, LANES)), _resident((1, 1))],
            out_specs=blk,
            out_shape=out_shape,
            scratch_shapes=[pltpu.VMEM((seq, 2 * LANES), BF16)],
            compiler_params=params,
            name="diff_attention",
        )(*args, bound)

    def online(args):
        return pl.pallas_call(
            functools.partial(_attn_online_body, lambda_init=lambda_init),
            grid=(batch, heads),
            in_specs=[blk, blk, blk, vec, vec, vec, vec, _resident((1, LANES))],
            out_specs=blk,
            out_shape=out_shape,
            scratch_shapes=[
                pltpu.VMEM((2 * T, 1), F32),
                pltpu.VMEM((2 * T, 1), F32),
                pltpu.VMEM((2 * T, LANES), F32),
            ],
            compiler_params=params,
            name="diff_attention_online",
        )(*args)

    args = (q, k, v, lq1, lk1, lq2, lk2, subw)
    return lax.cond(bound[0, 0] <= ATT_FIXED_SHIFT_MAX, fixed_shift, online, args)


def _outffn_body(x_ref, ys_ref, ya_ref, wout_ref, n2_ref, wg_ref, wu_ref, wd_ref, o_ref):
    ycat = jnp.concatenate([ys_ref[...]] + [ya_ref[c] for c in range(ATT_HEADS)], axis=1)
    x1 = x_ref[...] + _dot(ycat, wout_ref[...])
    ms = jnp.mean(x1 * x1, axis=-1, keepdims=True)
    h = (x1 * lax.rsqrt(ms + EPS) * n2_ref[...]).astype(BF16)
    out = x1
    for lo, hi in FFN_SPLITS:
        gate = _dot(h, wg_ref[:, lo:hi])
        up = _dot(h, wu_ref[:, lo:hi])
        act = (_silu(gate) * up).astype(BF16)
        out = out + _dot(act, wd_ref[lo:hi, :])
    o_ref[...] = out


def _out_ffn(x2, ys, ya, wout, n2, wg, wu, wd, batch, seq):
    m = x2.shape[0]
    tm = min(FFN_ROWS, seq)
    tiles_per_seq = seq // tm
    row = lambda i: (i, 0)
    return pl.pallas_call(
        _outffn_body,
        grid=(m // tm,),
        in_specs=[
            pl.BlockSpec((tm, D_MODEL), row),
            pl.BlockSpec((tm, SSD_WIDTH), row),
            pl.BlockSpec((None, ATT_HEADS, tm, LANES),
                         lambda i: (i // tiles_per_seq, 0, i % tiles_per_seq, 0)),
            _resident(wout.shape), _resident((1, D_MODEL)),
            _resident(wg.shape), _resident(wu.shape), _resident(wd.shape),
        ],
        out_specs=pl.BlockSpec((tm, D_MODEL), row),
        out_shape=jax.ShapeDtypeStruct((m, D_MODEL), F32),
        compiler_params=pltpu.CompilerParams(
            dimension_semantics=("arbitrary",), vmem_limit_bytes=VMEM_LIMIT_BYTES),
        name="out_ffn",
    )(x2, ys, ya, wout, n2, wg, wu, wd)


def _rope_tables(seq, norm_w, scale):
    half = ROT_DIM // 2
    pos = jnp.arange(seq, dtype=F32)
    inv_freq = ROPE_THETA ** (-jnp.arange(0, ROT_DIM, 2, dtype=F32) / ROT_DIM)
    ang = pos[:, None] * inv_freq[None, :]
    cos, sin = jnp.cos(ang), jnp.sin(ang)
    rest = ATT_HEAD_DIM - ROT_DIM
    c64 = jnp.concatenate([cos, cos, jnp.ones((seq, rest), F32)], axis=1)
    s1 = jnp.concatenate([jnp.zeros((seq, half), F32), sin, jnp.zeros((seq, rest), F32)], axis=1)
    s2 = jnp.concatenate([-sin, jnp.zeros((seq, ATT_HEAD_DIM - half), F32)], axis=1)
    w = norm_w.astype(F32) * scale
    cw = c64 * w[None, :]
    s1w = s1 * jnp.roll(w, half)[None, :]
    s2w = s2 * jnp.roll(w, -half)[None, :]
    return tuple(jnp.tile(t, (1, LANES // ATT_HEAD_DIM)) for t in (cw, s1w, s2w))


def _pad_lanes(v, width=LANES):
    v = v.reshape(1, -1).astype(F32)
    return jnp.pad(v, ((0, 0), (0, width - v.shape[1])))


def _layer(x, l, norm1_w, w_in, conv_w, conv_b, dt_bias, a_log, d_skip, ssd_norm_w,
           q_norm_w, k_norm_w, lambda_q1, lambda_k1, lambda_q2, lambda_k2, subln_w,
           w_out, norm2_w, w_gate, w_up, w_down):
    batch, seq, _ = x.shape
    lambda_init = 0.8 - 0.6 * math.exp(-0.3 * l)
    x2 = x.reshape(batch * seq, D_MODEL)

    o_xbc = SSD_WIDTH
    o_dt = o_xbc + CONV_DIM
    o_q = o_dt + SSD_HEADS
    o_k = o_q + ATT_WIDTH
    o_v = o_k + ATT_WIDTH
    wb = w_in.astype(BF16)
    wz, wxbc = wb[:, :o_xbc], wb[:, o_xbc:o_dt]
    wdt = jnp.pad(wb[:, o_dt:o_q], ((0, 0), (0, LANES - SSD_HEADS)))
    wq, wk, wv = wb[:, o_q:o_k], wb[:, o_k:o_v], wb[:, o_v:]

    qtabs = _rope_tables(seq, q_norm_w, ATT_HEAD_DIM ** -0.5)
    ktabs = _rope_tables(seq, k_norm_w, 1.0)
    blk = jnp.arange(LANES) // ATT_HEAD_DIM
    bd = jnp.where(blk[:, None] == blk[None, :], 1.0 / ATT_HEAD_DIM, 0.0).astype(BF16)

    z, xbc, dt, q, k, v = _in_projection(
        x2, norm1_w.reshape(1, -1), wz, wxbc, wdt, wq, wk, wv, _pad_lanes(dt_bias),
        qtabs, ktabs, bd, batch, seq)

    y_ssd = _ssd_mixer(
        xbc, z, dt, conv_w.astype(F32), conv_b.reshape(1, -1).astype(F32), _pad_lanes(a_log),
        jnp.repeat(d_skip.astype(F32), SSD_HEAD_DIM).reshape(1, -1),
        ssd_norm_w.reshape(1, -1).astype(F32), batch, seq)

    vec = lambda t: t.reshape(1, -1).astype(F32)
    y_att = _diff_attention(q, k, v, vec(lambda_q1), vec(lambda_k1), vec(lambda_q2),
                            vec(lambda_k2), vec(subln_w), q_norm_w, k_norm_w, lambda_init)

    out = _out_ffn(x2, y_ssd, y_att, w_out.astype(BF16), norm2_w.reshape(1, -1).astype(F32),
                   w_gate.astype(BF16), w_up.astype(BF16), w_down.astype(BF16), batch, seq)
    return out.reshape(batch, seq, D_MODEL)


def kernel(x, norm1_w, w_in, conv_w, conv_b, dt_bias, a_log, d_skip, ssd_norm_w, q_norm_w, k_norm_w, lambda_q1, lambda_k1, lambda_q2, lambda_k2, subln_w, w_out, norm2_w, w_gate, w_up, w_down):
    depth = w_in.shape[0]
    for l in range(depth):
        x = _layer(x, l, norm1_w[l], w_in[l], conv_w[l], conv_b[l], dt_bias[l], a_log[l],
                   d_skip[l], ssd_norm_w[l], q_norm_w[l], k_norm_w[l], lambda_q1[l], lambda_k1[l],
                   lambda_q2[l], lambda_k2[l], subln_w[l], w_out[l], norm2_w[l], w_gate[l],
                   w_up[l], w_down[l])
    return x
```

```python
import functools
import math

import numpy as np

import jax
import jax.numpy as jnp
from jax import lax
from jax.experimental import pallas as pl
from jax.experimental.pallas import tpu as pltpu

F32 = jnp.float32
BF16 = jnp.bfloat16

D_MODEL = 1024
SSD_WIDTH = 1024
SSD_HEAD_DIM = 64
SSD_HEADS = 16
SSD_GROUPS = 2
SSD_STATE = 128
SSD_HEADS_PER_GROUP = SSD_HEADS // SSD_GROUPS
CONV_WIDTH = 4
CONV_DIM = SSD_WIDTH + 2 * SSD_GROUPS * SSD_STATE
ATT_WIDTH = 1024
ATT_HEAD_DIM = 64
ATT_HEADS = 8
ATT_V_DIM = 2 * ATT_HEAD_DIM
ROT_DIM = ATT_HEAD_DIM // 4
ROPE_THETA = 500000.0
MASK_CHUNK = 64
D_MIX = SSD_WIDTH + ATT_WIDTH
D_FF = 2816
EPS = 1e-6

LANES = 128
SUBLANES = 8
VMEM_LIMIT_BYTES = 56 * 1024 * 1024

PROJ_ROWS = 512
SSD_CHUNK = 128
ATT_BLOCK = 256
FFN_ROWS = 512
FFN_SPLITS = ((0, 1536), (1536, 2816))
PIECE_SCHEDULE = (2, 3, 2, 1, 0)


def _resident(shape):
    nd = len(shape)
    return pl.BlockSpec(shape, lambda *_: (0,) * nd, pipeline_mode=pl.Buffered(1))


def _silu(x):
    return x / (1.0 + jnp.exp(-x))


def _softplus(x):
    return jnp.maximum(x, 0.0) + jnp.log(1.0 + jnp.exp(-jnp.abs(x)))


def _split3(a):
    hi = a.astype(BF16)
    r1 = a - hi.astype(F32)
    mid = r1.astype(BF16)
    lo = (r1 - mid.astype(F32)).astype(BF16)
    return hi, mid, lo


def _dot(a, b):
    return jnp.dot(a, b, preferred_element_type=F32)


def _dot_nt(a, b):
    return lax.dot_general(a, b, (((1,), (1,)), ((), ())), preferred_element_type=F32)


def _qk_post(t, cw, sw, bd):
    outs = []
    for c in range(t.shape[1] // LANES):
        tc = t[:, c * LANES:(c + 1) * LANES]
        ms = _dot((tc * tc).astype(BF16), bd)
        r = lax.rsqrt(ms + EPS)
        rot = tc * cw + pltpu.roll(tc, LANES // 2, 1) * sw
        outs.append((rot * r).astype(BF16))
    return outs


def _ssd_prep(xc, dt, a):
    L = xc.shape[0]
    N = SSD_STATE
    adt = dt * a
    rows = lax.broadcasted_iota(jnp.int32, (L, L), 0)
    cols = lax.broadcasted_iota(jnp.int32, (L, L), 1)
    causal = rows >= cols
    tri = jnp.where(causal, 1.0, 0.0).astype(BF16)
    hi, mid, lo = _split3(adt)
    a_cum = _dot(tri, hi) + _dot(tri, mid) + _dot(tri, lo)
    a_last = a_cum[L - 1:L, :]
    groups = []
    for g in range(SSD_GROUPS):
        bg = xc[:, SSD_WIDTH + g * N:SSD_WIDTH + (g + 1) * N]
        cg = xc[:, SSD_WIDTH + SSD_GROUPS * N + g * N:SSD_WIDTH + SSD_GROUPS * N + (g + 1) * N]
        groups.append(dict(cg=cg, cb=_dot_nt(cg.astype(BF16), bg.astype(BF16)),
                           bg_t16=bg.T.astype(BF16)))
    return dict(causal=causal, a_cum=a_cum, a_cum_t=a_cum.T, dt_t=dt.T,
                wgt=dt * jnp.exp(a_last - a_cum),
                e_last=jnp.exp(a_last), groups=groups)


def _ssd_lhs(prep):
    causal, a_cum, a_cum_t, dt_t = prep["causal"], prep["a_cum"], prep["a_cum_t"], prep["dt_t"]
    out = []
    for g in range(SSD_GROUPS):
        cg, cb = prep["groups"][g]["cg"], prep["groups"][g]["cb"]
        for pr in range(SSD_HEADS_PER_GROUP // 2):
            h0 = g * SSD_HEADS_PER_GROUP + 2 * pr
            rows = []
            for hh in (h0, h0 + 1):
                col = a_cum[:, hh:hh + 1]
                row = a_cum_t[hh:hh + 1, :]
                dec = jnp.exp(jnp.where(causal, col - row, -jnp.inf))
                mh = cb * dec * dt_t[hh:hh + 1, :]
                ce = cg * jnp.exp(col)
                rows.append(jnp.concatenate([mh.astype(BF16), ce.astype(BF16)], axis=1))
            out.append(jnp.concatenate(rows, axis=0))
    return out


def _ssd_pairs(prep, lhs, xc, state):
    L = xc.shape[0]
    wgt, e_last = prep["wgt"], prep["e_last"]
    lane = lax.broadcasted_iota(jnp.int32, (1, LANES), 1)
    lo_half = lane < SSD_HEAD_DIM
    y_tiles = []
    for g in range(SSD_GROUPS):
        st = state[g]
        st16 = st.astype(BF16)
        xw_parts = []
        dec_parts = []
        for pr in range(SSD_HEADS_PER_GROUP // 2):
            h0 = g * SSD_HEADS_PER_GROUP + 2 * pr
            xs_pair = xc[:, h0 * SSD_HEAD_DIM:(h0 + 2) * SSD_HEAD_DIM]
            rhs = jnp.concatenate(
                [xs_pair.astype(BF16), st16[:, pr * LANES:(pr + 1) * LANES]], axis=0)
            yy = _dot(lhs[g * (SSD_HEADS_PER_GROUP // 2) + pr], rhs)
            y_tiles.append(jnp.where(lo_half, yy[:L], yy[L:]))
            wb = jnp.where(lo_half, wgt[:, h0:h0 + 1], wgt[:, h0 + 1:h0 + 2])
            xw_parts.append((xs_pair * wb).astype(BF16))
            dec_parts.append(jnp.where(lo_half, e_last[:, h0:h0 + 1], e_last[:, h0 + 1:h0 + 2]))
        xw = jnp.concatenate(xw_parts, axis=1)
        state[g] = (st * jnp.concatenate(dec_parts, axis=1)
                    + _dot(prep["groups"][g]["bg_t16"], xw))
    return y_tiles


def _ssd_gate(y_tiles, xc, z, dskip, normw):
    GW = SSD_WIDTH // SSD_GROUPS
    y = jnp.concatenate(y_tiles, axis=1) + dskip * xc[:, :SSD_WIDTH]
    gz = y * _silu(z)
    outs = []
    for g in range(SSD_GROUPS):
        gg = gz[:, g * GW:(g + 1) * GW]
        ms = jnp.mean(gg * gg, axis=-1, keepdims=True)
        outs.append(gg * lax.rsqrt(ms + EPS))
    return (jnp.concatenate(outs, axis=1) * normw).astype(BF16)


def _inproj_ssd_body(x_ref, n1_ref, wz_ref, wxbc_ref, wdt_ref, wq_ref, wk_ref, wv_ref, dtb_ref,
                     qc_ref, qs_ref, kc_ref, ks_ref, bd_ref,
                     convw_ref, convb_ref, alog_ref, dskip_ref, normw_ref,
                     y_ref, q_ref, k_ref, v_ref, convtail, state):
    T = x_ref.shape[0]
    L = min(SSD_CHUNK, T)

    @pl.when(pl.program_id(1) == 0)
    def _init():
        convtail[...] = jnp.zeros_like(convtail)
        state[...] = jnp.zeros_like(state)

    x = x_ref[...]
    ms = jnp.mean(x * x, axis=-1, keepdims=True)
    h = (x * lax.rsqrt(ms + EPS) * n1_ref[...]).astype(BF16)

    lane = lax.broadcasted_iota(jnp.int32, (1, LANES), 1)
    dt = _softplus(_dot(h, wdt_ref[...]) + dtb_ref[...])
    dt = jnp.where(lane < SSD_HEADS, dt, 0.0)
    a = -jnp.exp(alog_ref[...])
    bd = bd_ref[...]
    PW = 4 * LANES

    nchunks = T // L
    z_parts = []

    def z_piece(i):
        def run():
            z_parts.append(_dot(h, wz_ref[:, i * PW:(i + 1) * PW]))
        return run

    def qk_piece(w_ref, out_ref, tabs, i):
        def run():
            t = _dot(h, w_ref[:, i * PW:(i + 1) * PW])
            for j, tile in enumerate(_qk_post(t, tabs[0][...], tabs[1][...], bd)):
                out_ref[(PW // LANES) * i + j] = tile
        return run

    def v_piece(i):
        def run():
            t = _dot(h, wv_ref[:, i * PW:(i + 1) * PW]).astype(BF16)
            for j in range(PW // LANES):
                v_ref[(PW // LANES) * i + j] = t[:, j * LANES:(j + 1) * LANES]
        return run

    npieces = ATT_WIDTH // PW
    qtabs = (qc_ref, qs_ref)
    ktabs = (kc_ref, ks_ref)
    pieces = ([z_piece(i) for i in range(SSD_WIDTH // PW)]
              + [qk_piece(wq_ref, q_ref, qtabs, i) for i in range(npieces)]
              + [qk_piece(wk_ref, k_ref, ktabs, i) for i in range(npieces)]
              + [v_piece(i) for i in range(npieces)])
    schedule = PIECE_SCHEDULE
    assert sum(schedule) == len(pieces) and len(schedule) == nchunks + 1
    xbc = _dot(h, wxbc_ref[...])
    u = None
    for k in range(CONV_WIDTH - 1):
        t = convw_ref[k:k + 1, :] * xbc
        if u is not None:
            t = t + u
        ext = jnp.concatenate([convtail[k], t], axis=0)
        convtail[k] = t[T - SUBLANES:, :]
        u = pltpu.roll(ext, 1, 0)[SUBLANES:, :]
    acc = convw_ref[CONV_WIDTH - 1:CONV_WIDTH, :] * xbc + u + convb_ref[...]
    xc = _silu(acc)
    preps = [_ssd_prep(xc[c * L:(c + 1) * L], dt[c * L:(c + 1) * L], a) for c in range(nchunks)]

    for f in pieces[:schedule[0]]:
        f()
    done = schedule[0]
    lhs = _ssd_lhs(preps[0])
    for c in range(nchunks):
        sl = slice(c * L, (c + 1) * L)
        y_tiles = _ssd_pairs(preps[c], lhs, xc[sl], state)
        if c + 1 < nchunks:
            lhs = _ssd_lhs(preps[c + 1])
        for f in pieces[done:done + schedule[c + 1]]:
            f()
        done += schedule[c + 1]
        z = jnp.concatenate(z_parts, axis=1)[sl]
        y_ref[sl, :] = _ssd_gate(y_tiles, xc[sl], z, dskip_ref[...], normw_ref[...])


def _inproj_ssd(x2, n1, wz, wxbc, wdt, wq, wk, wv, dtb, qtabs, ktabs, bd,
                convw, convb, alog, dskip, normw, batch, seq):
    m = x2.shape[0]
    tm = min(PROJ_ROWS, seq)
    nt = seq // tm
    row = lambda b, t: (b * nt + t, 0)
    tab = pl.BlockSpec((tm, LANES), lambda b, t: (t, 0))
    head_out = pl.BlockSpec((None, ATT_HEADS, tm, LANES), lambda b, t: (b, 0, t, 0))
    hshape = jax.ShapeDtypeStruct((batch, ATT_HEADS, seq, LANES), BF16)
    return pl.pallas_call(
        _inproj_ssd_body,
        grid=(batch, nt),
        in_specs=[
            pl.BlockSpec((tm, D_MODEL), row),
            _resident((1, D_MODEL)),
            _resident(wz.shape), _resident(wxbc.shape), _resident(wdt.shape),
            _resident(wq.shape), _resident(wk.shape), _resident(wv.shape),
            _resident((1, LANES)),
            tab, tab, tab, tab,
            _resident((LANES, LANES)),
            _resident((CONV_WIDTH, CONV_DIM)),
            _resident((1, CONV_DIM)),
            _resident((1, LANES)),
            _resident((1, SSD_WIDTH)),
            _resident((1, SSD_WIDTH)),
        ],
        out_specs=[pl.BlockSpec((tm, SSD_WIDTH), row), head_out, head_out, head_out],
        out_shape=[jax.ShapeDtypeStruct((m, SSD_WIDTH), BF16), hshape, hshape, hshape],
        scratch_shapes=[
            pltpu.VMEM((CONV_WIDTH - 1, SUBLANES, CONV_DIM), F32),
            pltpu.VMEM((SSD_GROUPS, SSD_STATE, SSD_WIDTH // SSD_GROUPS), F32),
        ],
        compiler_params=pltpu.CompilerParams(
            dimension_semantics=("arbitrary", "arbitrary"), vmem_limit_bytes=VMEM_LIMIT_BYTES),
        name="inproj_ssd",
    )(x2, n1, wz, wxbc, wdt, wq, wk, wv, dtb, *qtabs, *ktabs, bd,
      convw, convb, alog, dskip, normw)


ATT_SCORE_BOUND_COEF = ATT_HEAD_DIM * ATT_HEAD_DIM ** -0.5
ATT_FIXED_SHIFT_MAX = 40.0


def _attn_lambda(lq1_ref, lk1_ref, lq2_ref, lk2_ref, lambda_init):
    return (jnp.exp(jnp.sum(lq1_ref[...] * lk1_ref[...], axis=-1, keepdims=True))
            - jnp.exp(jnp.sum(lq2_ref[...] * lk2_ref[...], axis=-1, keepdims=True))
            + lambda_init)


def _stack_subheads(qb):
    lane = lax.broadcasted_iota(jnp.int32, (1, LANES), 1)
    sub0 = (lane & (ATT_HEAD_DIM // 2)) == 0
    zero = jnp.zeros_like(qb)
    return jnp.concatenate([jnp.where(sub0, qb, zero), jnp.where(sub0, zero, qb)], axis=0)


def _diag_mask(T):
    r_chunk = (lax.broadcasted_iota(jnp.int32, (2 * T, T), 0) % T) // MASK_CHUNK
    c_chunk = lax.broadcasted_iota(jnp.int32, (2 * T, T), 1) // MASK_CHUNK
    return c_chunk <= r_chunk


def _attn_finish(o2, lam, subw, lambda_init, T):
    o = o2[:T] - lam * o2[T:]
    ms = jnp.mean(o * o, axis=-1, keepdims=True)
    return (o * lax.rsqrt(ms + EPS) * subw * (1.0 - lambda_init)).astype(BF16)


def _attn_fixed_shift_body(q_ref, k_ref, v_ref, lq1_ref, lk1_ref, lq2_ref, lk2_ref, subw_ref,
                           shift_ref, o_ref, v1_sc, *, lambda_init):
    S = q_ref.shape[0]
    T = min(ATT_BLOCK, S)
    nq = S // T
    lam = _attn_lambda(lq1_ref, lk1_ref, lq2_ref, lk2_ref, lambda_init)
    shift = shift_ref[...]
    mask = _diag_mask(T)
    v1_sc[:, :LANES] = v_ref[...]
    v1_sc[:, LANES:] = jnp.ones((S, LANES), BF16)
    for qi in range(nq):
        q2 = _stack_subheads(q_ref[qi * T:(qi + 1) * T, :])
        acc = jnp.zeros((2 * T, 2 * LANES), F32)
        for j in range(qi + 1):
            s = _dot_nt(q2, k_ref[j * T:(j + 1) * T, :])
            p = jnp.exp(s - shift)
            if j == qi:
                p = jnp.where(mask, p, 0.0)
            acc = acc + _dot(p.astype(BF16), v1_sc[j * T:(j + 1) * T, :])
        o2 = acc[:, :LANES] / acc[:, LANES:]
        o_ref[qi * T:(qi + 1) * T, :] = _attn_finish(o2, lam, subw_ref[...], lambda_init, T)


def _attn_online_body(q_ref, k_ref, v_ref, lq1_ref, lk1_ref, lq2_ref, lk2_ref, subw_ref, o_ref,
                      m_sc, l_sc, acc_sc, *, lambda_init):
    S = q_ref.shape[0]
    T = min(ATT_BLOCK, S)
    nq = S // T
    lam = _attn_lambda(lq1_ref, lk1_ref, lq2_ref, lk2_ref, lambda_init)
    mask = _diag_mask(T)

    def step(q2, kv_start, masked):
        kb = k_ref[pl.ds(kv_start, T), :]
        vb = v_ref[pl.ds(kv_start, T), :]
        s = _dot_nt(q2, kb)
        if masked:
            s = jnp.where(mask, s, -jnp.inf)
        m_prev = m_sc[...]
        m_new = jnp.maximum(m_prev, jnp.max(s, axis=-1, keepdims=True))
        alpha = jnp.exp(m_prev - m_new)
        p = jnp.exp(s - m_new)
        l_sc[...] = alpha * l_sc[...] + jnp.sum(p, axis=-1, keepdims=True)
        acc_sc[...] = alpha * acc_sc[...] + _dot(p.astype(BF16), vb)
        m_sc[...] = m_new

    def q_block(qi, carry):
        q_start = pl.multiple_of(qi * T, T)
        q2 = _stack_subheads(q_ref[pl.ds(q_start, T), :])
        m_sc[...] = jnp.full(m_sc.shape, -jnp.inf, F32)
        l_sc[...] = jnp.zeros(l_sc.shape, F32)
        acc_sc[...] = jnp.zeros(acc_sc.shape, F32)

        def kv_block(j, c2):
            step(q2, pl.multiple_of(j * T, T), False)
            return c2

        lax.fori_loop(0, qi, kv_block, 0)
        step(q2, q_start, True)
        o2 = acc_sc[...] / l_sc[...]
        o_ref[pl.ds(q_start, T), :] = _attn_finish(o2, lam, subw_ref[...], lambda_init, T)
        return carry

    lax.fori_loop(0, nq, q_block, 0)


def _diff_attention(q, k, v, lq1, lk1, lq2, lk2, subw, q_norm_w, k_norm_w, lambda_init):
    batch, heads, seq, _ = q.shape
    T = min(ATT_BLOCK, seq)
    blk = pl.BlockSpec((None, None, seq, LANES), lambda b, h: (b, h, 0, 0))
    vec = _resident((1, ATT_HEAD_DIM))
    params = pltpu.CompilerParams(
        dimension_semantics=("arbitrary", "arbitrary"), vmem_limit_bytes=VMEM_LIMIT_BYTES)
    out_shape = jax.ShapeDtypeStruct(q.shape, BF16)
    bound = ATT_SCORE_BOUND_COEF * jnp.max(jnp.abs(q_norm_w)) * jnp.max(jnp.abs(k_norm_w))
    bound = bound.astype(F32).reshape(1, 1)

    def fixed_shift(args):
        return pl.pallas_call(
            functools.partial(_attn_fixed_shift_body, lambda_init=lambda_init),
            grid=(batch, heads),
            in_specs=[blk, blk, blk, vec, vec, vec, vec, _resident((1, LANES)), _resident((1, 1))],
            out_specs=blk,
            out_shape=out_shape,
            scratch_shapes=[pltpu.VMEM((seq, 2 * LANES), BF16)],
            compiler_params=params,
            name="diff_attention",
        )(*args, bound)

    def online(args):
        return pl.pallas_call(
            functools.partial(_attn_online_body, lambda_init=lambda_init),
            grid=(batch, heads),
            in_specs=[blk, blk, blk, vec, vec, vec, vec, _resident((1, LANES))],
            out_specs=blk,
            out_shape=out_shape,
            scratch_shapes=[
                pltpu.VMEM((2 * T, 1), F32),
                pltpu.VMEM((2 * T, 1), F32),
                pltpu.VMEM((2 * T, LANES), F32),
            ],
            compiler_params=params,
            name="diff_attention_online",
        )(*args)

    args = (q, k, v, lq1, lk1, lq2, lk2, subw)
    return lax.cond(bound[0, 0] <= ATT_FIXED_SHIFT_MAX, fixed_shift, online, args)


def _outffn_body(x_ref, ys_ref, ya_ref, wout_ref, n2_ref, wg_ref, wu_ref, wd_ref, o_ref):
    ycat = jnp.concatenate([ys_ref[...]] + [ya_ref[c] for c in range(ATT_HEADS)], axis=1)
    x1 = x_ref[...] + _dot(ycat, wout_ref[...])
    ms = jnp.mean(x1 * x1, axis=-1, keepdims=True)
    h = (x1 * lax.rsqrt(ms + EPS) * n2_ref[...]).astype(BF16)
    out = x1
    for lo, hi in FFN_SPLITS:
        gate = _dot(h, wg_ref[:, lo:hi])
        up = _dot(h, wu_ref[:, lo:hi])
        act = (_silu(gate) * up).astype(BF16)
        out = out + _dot(act, wd_ref[lo:hi, :])
    o_ref[...] = out


def _out_ffn(x2, ys, ya, wout, n2, wg, wu, wd, batch, seq):
    m = x2.shape[0]
    tm = min(FFN_ROWS, seq)
    tiles_per_seq = seq // tm
    row = lambda i: (i, 0)
    return pl.pallas_call(
        _outffn_body,
        grid=(m // tm,),
        in_specs=[
            pl.BlockSpec((tm, D_MODEL), row),
            pl.BlockSpec((tm, SSD_WIDTH), row),
            pl.BlockSpec((None, ATT_HEADS, tm, LANES),
                         lambda i: (i // tiles_per_seq, 0, i % tiles_per_seq, 0)),
            _resident(wout.shape), _resident((1, D_MODEL)),
            _resident(wg.shape), _resident(wu.shape), _resident(wd.shape),
        ],
        out_specs=pl.BlockSpec((tm, D_MODEL), row),
        out_shape=jax.ShapeDtypeStruct((m, D_MODEL), F32),
        compiler_params=pltpu.CompilerParams(
            dimension_semantics=("arbitrary",), vmem_limit_bytes=VMEM_LIMIT_BYTES),
        name="out_ffn",
    )(x2, ys, ya, wout, n2, wg, wu, wd)


def _head_lane_layout():
    lam = np.arange(LANES)
    half, j, i = lam // 64, (lam % 64) // 32, lam % 32
    d = np.where(i < ROT_DIM // 2, np.where(half == 0, i, ROT_DIM // 2 + i),
                 np.where(half == 0, i + ROT_DIM // 2, i + 32))
    return half, j, i, d


def _rope_tables(seq, norm_w, scale):
    half, j, i, d = _head_lane_layout()
    d_partner = d[(np.arange(LANES) + LANES // 2) % LANES]
    pos = jnp.arange(seq, dtype=F32)
    inv_freq = ROPE_THETA ** (-jnp.arange(0, ROT_DIM, 2, dtype=F32) / ROT_DIM)
    ang = pos[:, None] * inv_freq[None, :]
    cos, sin = jnp.cos(ang), jnp.sin(ang)
    rot = i < ROT_DIM // 2
    fi = np.where(rot, i, 0)
    c = jnp.where(rot[None, :], cos[:, fi], 1.0)
    sgn = np.where(half == 0, -1.0, 1.0).astype(np.float32)
    sn = jnp.where(rot[None, :], sin[:, fi] * sgn[None, :], 0.0)
    w = norm_w.astype(F32) * scale
    return c * w[d][None, :], sn * w[d_partner][None, :]


def _pad_lanes(v, width=LANES):
    v = v.reshape(1, -1).astype(F32)
    return jnp.pad(v, ((0, 0), (0, width - v.shape[1])))


def _layer(x, l, norm1_w, w_in, conv_w, conv_b, dt_bias, a_log, d_skip, ssd_norm_w,
           q_norm_w, k_norm_w, lambda_q1, lambda_k1, lambda_q2, lambda_k2, subln_w,
           w_out, norm2_w, w_gate, w_up, w_down):
    batch, seq, _ = x.shape
    lambda_init = 0.8 - 0.6 * math.exp(-0.3 * l)
    x2 = x.reshape(batch * seq, D_MODEL)

    o_xbc = SSD_WIDTH
    o_dt = o_xbc + CONV_DIM
    o_q = o_dt + SSD_HEADS
    o_k = o_q + ATT_WIDTH
    o_v = o_k + ATT_WIDTH
    wb = w_in.astype(BF16)
    wz, wxbc = wb[:, :o_xbc], wb[:, o_xbc:o_dt]
    wdt = jnp.pad(wb[:, o_dt:o_q], ((0, 0), (0, LANES - SSD_HEADS)))
    half, j, i, d = _head_lane_layout()
    perm = j * ATT_HEAD_DIM + d
    head_perm = lambda w: w.reshape(D_MODEL, ATT_HEADS, LANES)[:, :, perm].reshape(D_MODEL, ATT_WIDTH)
    wq, wk, wv = head_perm(wb[:, o_q:o_k]), head_perm(wb[:, o_k:o_v]), wb[:, o_v:]

    qtabs = _rope_tables(seq, q_norm_w, ATT_HEAD_DIM ** -0.5)
    ktabs = _rope_tables(seq, k_norm_w, 1.0)
    bd = jnp.asarray(np.where(j[:, None] == j[None, :], 1.0 / ATT_HEAD_DIM, 0.0), BF16)

    y_ssd, q, k, v = _inproj_ssd(
        x2, norm1_w.reshape(1, -1), wz, wxbc, wdt, wq, wk, wv, _pad_lanes(dt_bias),
        qtabs, ktabs, bd, conv_w.astype(F32), conv_b.reshape(1, -1).astype(F32), _pad_lanes(a_log),
        jnp.repeat(d_skip.astype(F32), SSD_HEAD_DIM).reshape(1, -1),
        ssd_norm_w.reshape(1, -1).astype(F32), batch, seq)

    vec = lambda t: t.reshape(1, -1).astype(F32)
    y_att = _diff_attention(q, k, v, vec(lambda_q1), vec(lambda_k1), vec(lambda_q2),
                            vec(lambda_k2), vec(subln_w), q_norm_w, k_norm_w, lambda_init)

    out = _out_ffn(x2, y_ssd, y_att, w_out.astype(BF16), norm2_w.reshape(1, -1).astype(F32),
                   w_gate.astype(BF16), w_up.astype(BF16), w_down.astype(BF16), batch, seq)
    return out.reshape(batch, seq, D_MODEL)


def kernel(x, norm1_w, w_in, conv_w, conv_b, dt_bias, a_log, d_skip, ssd_norm_w, q_norm_w, k_norm_w, lambda_q1, lambda_k1, lambda_q2, lambda_k2, subln_w, w_out, norm2_w, w_gate, w_up, w_down):
    depth = w_in.shape[0]
    for l in range(depth):
        x = _layer(x, l, norm1_w[l], w_in[l], conv_w[l], conv_b[l], dt_bias[l], a_log[l],
                   d_skip[l], ssd_norm_w[l], q_norm_w[l], k_norm_w[l], lambda_q1[l], lambda_k1[l],
                   lambda_q2[l], lambda_k2[l], subln_w[l], w_out[l], norm2_w[l], w_gate[l],
                   w_up[l], w_down[l])
    return x
```
